```python
import math
import jax, jax.numpy as jnp
from jax import lax
import numpy as np

D_MODEL = 2048
BATCH = 4
SEQ = 2048
DEPTH = 2
DEC_BATCH = 128
DEC_SEQ = 1
PAST_LEN = 8192
PAGE_SIZE = 128

D_FF = 5632
PLE_DIM = 256
MLA_HEADS = 8
Q_LORA = 512
KV_LORA = 512
QK_NOPE = 128
QK_ROPE = 64
V_HEAD = 128
ROPE_THETA = 10000.0
ATTN_BLOCK = 128
MLA_SCALE = (QK_NOPE + QK_ROPE) ** -0.5
SSM_HEADS = 16
SSM_HEADDIM = 64
SSM_D_INNER = SSM_HEADS * SSM_HEADDIM
SSM_GROUPS = 4
SSM_STATE = 128
CONV_W = 4
CONV_DIM = SSM_D_INNER + 2 * SSM_GROUPS * SSM_STATE
SSM_CHUNK = 128
MLSTM_HEADS = 4
MLSTM_QK = 128
MLSTM_V = 256
MLSTM_CHUNK = 128
N_BRANCH = 3
BRANCH_WIDTH = 1024
EPS = 1e-6
IN_SPLITS = (Q_LORA, KV_LORA, QK_ROPE,
             SSM_D_INNER, CONV_DIM, SSM_HEADS,
             MLSTM_HEADS * MLSTM_QK, MLSTM_HEADS * MLSTM_QK, MLSTM_HEADS * MLSTM_V,
             MLSTM_HEADS, MLSTM_HEADS, MLSTM_HEADS * MLSTM_V,
             N_BRANCH * D_MODEL)
D_IN = sum(IN_SPLITS)

kernel_name = 'hybrid_mla_ssd_mlstm_decoder_step'


def rmsnorm(x, g):
    xf = x.astype(jnp.float32)
    y = xf * lax.rsqrt(jnp.mean(xf * xf, axis=-1, keepdims=True) + EPS)
    return (y * g.astype(jnp.float32)).astype(x.dtype)


def split_cols(t, sizes):
    out, start = [], 0
    for s in sizes:
        out.append(t[..., start:start + s])
        start += s
    return out


def rope(x, pos):
    half = QK_ROPE // 2
    inv = ROPE_THETA ** (-jnp.arange(half, dtype=jnp.float32) / half)
    ang = pos.astype(jnp.float32)[:, None] * inv[None, :]
    shape = (1, pos.shape[0]) + (1,) * (x.ndim - 3) + (half,)
    cos, sin = jnp.cos(ang).reshape(shape), jnp.sin(ang).reshape(shape)
    x1, x2 = x[..., :half], x[..., half:]
    return jnp.concatenate([x1 * cos - x2 * sin, x2 * cos + x1 * sin], axis=-1).astype(x.dtype)


def swiglu(x, w_in, w_out):
    a, b = jnp.split(x @ w_in, 2, axis=-1)
    return (jax.nn.silu(a) * b) @ w_out


def mla_project(cq, ckv, kr_raw, pos, lw):
    B, L = cq.shape[:2]
    q = (rmsnorm(cq, lw['q_norm']) @ lw['w_uq']).reshape(B, L, MLA_HEADS, QK_NOPE + QK_ROPE)
    q_rope = rope(q[..., QK_NOPE:], pos)
    q_lat = jnp.einsum('blhd,rhd->blhr', q[..., :QK_NOPE], lw['w_uk'])
    c = rmsnorm(ckv, lw['kv_norm'])
    kr = rope(kr_raw, pos)
    return q_lat, q_rope, c, kr


def mla_scores(q_lat, q_rope, c, kr):
    s = jnp.einsum('bqhr,bkr->bhqk', q_lat, c) + jnp.einsum('bqhd,bkd->bhqk', q_rope, kr)
    return s.astype(jnp.float32) * MLA_SCALE


def mla_prompt(q_lat, q_rope, c, kr):
    B, S = c.shape[:2]
    blk = math.gcd(S, ATTN_BLOCK)
    nb = S // blk
    ql = q_lat.reshape(B, nb, blk, MLA_HEADS, KV_LORA).swapaxes(0, 1)
    qr = q_rope.reshape(B, nb, blk, MLA_HEADS, QK_ROPE).swapaxes(0, 1)
    k_pos = jnp.arange(S)

    def block(args):
        qlb, qrb, bi = args
        q_pos = bi * blk + jnp.arange(blk)
        s = mla_scores(qlb, qrb, c, kr)
        s = jnp.where(k_pos[None, :] <= q_pos[:, None], s, -jnp.inf)
        p = jax.nn.softmax(s, axis=-1).astype(c.dtype)
        return jnp.einsum('bhqk,bkr->bqhr', p, c)

    o = lax.map(block, (ql, qr, jnp.arange(nb)))
    return o.swapaxes(0, 1).reshape(B, S, MLA_HEADS, KV_LORA)


def mla_sample(q_lat, q_rope, c_new, kr_new, c_past, kr_past):
    L = c_new.shape[1]
    P = c_past.shape[1]
    s_past = mla_scores(q_lat, q_rope, c_past, kr_past)
    s_new = mla_scores(q_lat, q_rope, c_new, kr_new)
    s_new = jnp.where(jnp.tril(jnp.ones((L, L), bool)), s_new, -jnp.inf)
    p = jax.nn.softmax(jnp.concatenate([s_past, s_new], axis=-1), axis=-1).astype(c_new.dtype)
    return (jnp.einsum('bhqk,bkr->bqhr', p[..., :P], c_past)
            + jnp.einsum('bhqk,bkr->bqhr', p[..., P:], c_new))


def causal_conv(xbc, buf, w, b):
    L = xbc.shape[1]
    xp = jnp.concatenate([buf, xbc], axis=1)
    y = sum(xp[:, k:k + L] * w[k] for k in range(CONV_W)) + b
    return jax.nn.silu(y), xp[:, L:]


def ssd_scan(x, dt, A, Bm, Cm, S0):
    Bsz, L = x.shape[:2]
    cs = math.gcd(L, SSM_CHUNK)
    nc = L // cs
    rep = SSM_HEADS // SSM_GROUPS
    Bh = jnp.repeat(Bm, rep, axis=2)
    Ch = jnp.repeat(Cm, rep, axis=2)
    a = dt * A

    def chunk(t):
        return t.reshape((Bsz, nc, cs) + t.shape[2:])

    x_c, dt_c, B_c, C_c, a_c = chunk(x), chunk(dt), chunk(Bh), chunk(Ch), chunk(a)
    a_cum = jnp.cumsum(a_c, axis=2)
    causal = jnp.tril(jnp.ones((cs, cs), bool))[None, None, :, :, None]
    seg = a_cum[:, :, :, None, :] - a_cum[:, :, None, :, :]
    decay = jnp.exp(jnp.where(causal, seg, -jnp.inf))
    xdt = x_c * dt_c[..., None]
    y_diag = jnp.einsum('bclhn,bcshn,bclsh,bcshp->bclhp', C_c, B_c, decay, xdt)
    decay_out = jnp.exp(a_cum[:, :, -1:, :] - a_cum)
    chunk_states = jnp.einsum('bclhn,bclh,bclhp->bchpn', B_c, decay_out * dt_c, x_c).astype(jnp.float32)
    chunk_decay = jnp.exp(a_cum[:, :, -1, :])

    def step(S, inp):
        st, dc = inp
        return S * dc[:, :, None, None] + st, S

    S_final, S_prev = lax.scan(step, S0.astype(jnp.float32),
                               (chunk_states.swapaxes(0, 1), chunk_decay.swapaxes(0, 1)))
    S_prev = S_prev.swapaxes(0, 1)
    y_off = jnp.einsum('bclhn,bchpn,bclh->bclhp', C_c, S_prev, jnp.exp(a_cum))
    y = (y_diag + y_off).reshape(Bsz, L, SSM_HEADS, SSM_HEADDIM).astype(x.dtype)
    return y, S_final.astype(S0.dtype)


def mamba_branch(z, xbc, dt_raw, conv0, S0, lw):
    Bsz, L = z.shape[:2]
    xbc, conv_new = causal_conv(xbc, conv0, lw['conv_w'], lw['conv_b'])
    xs, Bm, Cm = split_cols(xbc, (SSM_D_INNER, SSM_GROUPS * SSM_STATE, SSM_GROUPS * SSM_STATE))
    xs = xs.reshape(Bsz, L, SSM_HEADS, SSM_HEADDIM)
    Bm = Bm.reshape(Bsz, L, SSM_GROUPS, SSM_STATE)
    Cm = Cm.reshape(Bsz, L, SSM_GROUPS, SSM_STATE)
    dt = jax.nn.softplus((dt_raw + lw['dt_bias']).astype(jnp.float32))
    A = -jnp.exp(lw['A_log'].astype(jnp.float32))
    y, S_new = ssd_scan(xs, dt, A, Bm, Cm, S0)
    y = (y + xs * lw['ssm_D'][:, None]).reshape(Bsz, L, SSM_D_INNER) * jax.nn.silu(z)
    y = rmsnorm(y.reshape(Bsz, L, SSM_GROUPS, -1), lw['ssm_norm'].reshape(SSM_GROUPS, -1))
    return y.reshape(Bsz, L, SSM_D_INNER), conv_new, S_new


def mlstm_scan(q, k, v, i_pre, logf, C0, n0, m0):
    Bsz, L = q.shape[:2]
    cs = math.gcd(L, MLSTM_CHUNK)
    nc = L // cs

    def chunk(t):
        return jnp.moveaxis(t.reshape((Bsz, nc, cs) + t.shape[2:]), 1, 0)

    qc, kc, vc, ic, fc = chunk(q * MLSTM_QK ** -0.5), chunk(k), chunk(v), chunk(i_pre), chunk(logf)
    causal = jnp.tril(jnp.ones((cs, cs), bool))[None, :, :, None]

    def step(carry, inp):
        C, n, m = carry
        qb, kb, vb, ib, fb = inp
        b = jnp.cumsum(fb, axis=1)
        D = b[:, :, None, :] - b[:, None, :, :] + ib[:, None, :, :]
        D = jnp.where(causal, D, -jnp.inf)
        inter = b + m[:, None, :]
        m_row = jnp.maximum(jnp.max(D, axis=2), inter)
        w = jnp.exp(D - m_row[:, :, None, :])
        w_inter = jnp.exp(inter - m_row)
        qk = jnp.einsum('bihd,bjhd->bijh', qb, kb).astype(jnp.float32) * w
        num = (jnp.einsum('bijh,bjhv->bihv', qk, vb)
               + w_inter[..., None] * jnp.einsum('bihd,bhdv->bihv', qb, C))
        den = jnp.sum(qk, axis=2) + w_inter * jnp.einsum('bihd,bhd->bih', qb, n)
        h = num / jnp.maximum(jnp.abs(den), jnp.exp(-m_row))[..., None]
        b_last = b[:, -1, :]
        g = b_last[:, None, :] - b + ib
        m_new = jnp.maximum(b_last + m, jnp.max(g, axis=1))
        wk = jnp.exp(g - m_new[:, None, :])
        wc = jnp.exp(b_last + m - m_new)
        C_new = wc[..., None, None] * C + jnp.einsum('bjh,bjhd,bjhv->bhdv', wk, kb, vb)
        n_new = wc[..., None] * n + jnp.einsum('bjh,bjhd->bhd', wk, kb)
        return (C_new, n_new, m_new), h

    f32 = jnp.float32
    (C, n, m), h = lax.scan(step, (C0.astype(f32), n0.astype(f32), m0.astype(f32)), (qc, kc, vc, ic, fc))
    h = jnp.moveaxis(h, 0, 1).reshape(Bsz, L, MLSTM_HEADS, MLSTM_V).astype(v.dtype)
    return h, C.astype(C0.dtype), n.astype(n0.dtype), m.astype(m0.dtype)


def mlstm_branch(mq, mk, mv, mi, mf, mo, C0, n0, m0, lw):
    Bsz, L = mq.shape[:2]
    q = mq.reshape(Bsz, L, MLSTM_HEADS, MLSTM_QK)
    k = mk.reshape(Bsz, L, MLSTM_HEADS, MLSTM_QK)
    v = mv.reshape(Bsz, L, MLSTM_HEADS, MLSTM_V)
    i_pre = (mi + lw['b_igate']).astype(jnp.float32)
    logf = jax.nn.log_sigmoid((mf + lw['b_fgate']).astype(jnp.float32))
    h, C, n, m = mlstm_scan(q, k, v, i_pre, logf, C0, n0, m0)
    h = rmsnorm(h, lw['mlstm_norm'].reshape(MLSTM_HEADS, MLSTM_V)).reshape(Bsz, L, MLSTM_HEADS * MLSTM_V)
    return jax.nn.sigmoid(mo) * h, C, n, m


def layer(h, p, pos, past, conv0, ssm0, C0, n0, m0, lw):
    Bsz, L = h.shape[:2]
    h = h + 0.5 * rmsnorm(swiglu(rmsnorm(h, lw['ffn1_pre']), lw['ffn1_in'], lw['ffn1_out']), lw['ffn1_post'])
    u = rmsnorm(h, lw['mix_pre'])
    (cq, ckv, kr_raw, z, xbc, dt_raw, mq, mk, mv, mi, mf, mo, gates) = split_cols(u @ lw['w_in'], IN_SPLITS)
    q_lat, q_rope, c, kr = mla_project(cq, ckv, kr_raw, pos, lw)
    if past is None:
        o_lat = mla_prompt(q_lat, q_rope, c, kr)
    else:
        o_lat = mla_sample(q_lat, q_rope, c, kr, past[0], past[1])
    ya = jnp.einsum('blhr,rhv->blhv', o_lat, lw['w_uv']).reshape(Bsz, L, MLA_HEADS * V_HEAD)
    yb, conv_new, ssm_new = mamba_branch(z, xbc, dt_raw, conv0, ssm0, lw)
    yc, C, n, m = mlstm_branch(mq, mk, mv, mi, mf, mo, C0, n0, m0, lw)
    y_br = jnp.einsum('blkc,kcd->blkd', jnp.stack([ya, yb, yc], axis=2), lw['w_branch'])
    g = jax.nn.sigmoid(gates).reshape(Bsz, L, N_BRANCH, D_MODEL)
    merged = jnp.sum(g * y_br, axis=2)
    h = h + rmsnorm(merged @ lw['w_o'], lw['mix_post'])
    h = h + 0.5 * rmsnorm(swiglu(rmsnorm(h, lw['ffn2_pre']), lw['ffn2_in'], lw['ffn2_out']), lw['ffn2_post'])
    gate = jax.nn.sigmoid(rmsnorm(h, lw['ple_pre']) @ lw['w_ple_gate'])
    h = h + rmsnorm(gate * (p @ lw['w_ple_proj']), lw['ple_post'])
    return h, (c, kr, ssm_new, conv_new, C, n, m)


def setup_inputs(seed: int = 0) -> dict:
    key = jax.random.key(seed)
    keys = iter(jax.random.split(key, 64))
    f32 = jnp.float32

    def normal(shape, scale):
        return scale * jax.random.normal(next(keys), shape, f32)

    def gain(shape):
        return 1.0 + 0.05 * jax.random.normal(next(keys), shape, f32)

    n_pages = PAST_LEN // PAGE_SIZE
    n_used = DEC_BATCH * n_pages
    n_pool = (5 * n_used + 3) // 4
    perm = jax.random.permutation(next(keys), n_pool)
    page_table = perm[:n_used].reshape(DEC_BATCH, n_pages).astype(jnp.int32)

    dt0 = jnp.exp(jax.random.uniform(next(keys), (DEPTH, SSM_HEADS), f32, math.log(1e-3), math.log(1e-1)))
    dt_bias = dt0 + jnp.log(-jnp.expm1(-dt0))
    A_log = jnp.log(jax.random.uniform(next(keys), (DEPTH, SSM_HEADS), f32, 1.0, 16.0))

    return {
        'x_prompt': normal((BATCH, SEQ, D_MODEL), 1.0),
        'x_sample': normal((DEC_BATCH, DEC_SEQ, D_MODEL), 1.0),
        'cache_kv_latent': normal((DEPTH, n_pool, PAGE_SIZE, KV_LORA), 1.0),
        'cache_k_rope': normal((DEPTH, n_pool, PAGE_SIZE, QK_ROPE), 1.0),
        'state_ssm': normal((DEPTH, DEC_BATCH, SSM_HEADS, SSM_HEADDIM, SSM_STATE), 0.5),
        'state_conv': normal((DEPTH, DEC_BATCH, CONV_W - 1, CONV_DIM), 1.0),
        'state_mlstm_c': normal((DEPTH, DEC_BATCH, MLSTM_HEADS, MLSTM_QK, MLSTM_V), 1.0),
        'state_mlstm_n': normal((DEPTH, DEC_BATCH, MLSTM_HEADS, MLSTM_QK), 1.0),
        'state_mlstm_m': normal((DEPTH, DEC_BATCH, MLSTM_HEADS), 0.5),
        'page_table': page_table,
        'p_prompt': normal((DEPTH, BATCH, SEQ, PLE_DIM), 1.0),
        'p_sample': normal((DEPTH, DEC_BATCH, DEC_SEQ, PLE_DIM), 1.0),
        'w_in': normal((DEPTH, D_MODEL, D_IN), D_MODEL ** -0.5),
        'w_uq': normal((DEPTH, Q_LORA, MLA_HEADS * (QK_NOPE + QK_ROPE)), Q_LORA ** -0.5),
        'w_uk': normal((DEPTH, KV_LORA, MLA_HEADS, QK_NOPE), KV_LORA ** -0.5),
        'w_uv': normal((DEPTH, KV_LORA, MLA_HEADS, V_HEAD), KV_LORA ** -0.5),
        'q_norm': gain((DEPTH, Q_LORA)),
        'kv_norm': gain((DEPTH, KV_LORA)),
        'conv_w': normal((DEPTH, CONV_W, CONV_DIM), CONV_W ** -0.5),
        'conv_b': normal((DEPTH, CONV_DIM), 0.01),
        'dt_bias': dt_bias,
        'A_log': A_log,
        'ssm_D': gain((DEPTH, SSM_HEADS)),
        'ssm_norm': gain((DEPTH, SSM_D_INNER)),
        'b_igate': normal((DEPTH, MLSTM_HEADS), 0.1),
        'b_fgate': 3.0 + normal((DEPTH, MLSTM_HEADS), 0.5),
        'mlstm_norm': gain((DEPTH, MLSTM_HEADS * MLSTM_V)),
        'w_branch': normal((DEPTH, N_BRANCH, BRANCH_WIDTH, D_MODEL), BRANCH_WIDTH ** -0.5),
        'w_o': normal((DEPTH, D_MODEL, D_MODEL), D_MODEL ** -0.5),
        'ffn1_pre': gain((DEPTH, D_MODEL)),
        'ffn1_in': normal((DEPTH, D_MODEL, 2 * D_FF), D_MODEL ** -0.5),
        'ffn1_out': normal((DEPTH, D_FF, D_MODEL), D_FF ** -0.5),
        'ffn1_post': gain((DEPTH, D_MODEL)),
        'mix_pre': gain((DEPTH, D_MODEL)),
        'mix_post': gain((DEPTH, D_MODEL)),
        'ffn2_pre': gain((DEPTH, D_MODEL)),
        'ffn2_in': normal((DEPTH, D_MODEL, 2 * D_FF), D_MODEL ** -0.5),
        'ffn2_out': normal((DEPTH, D_FF, D_MODEL), D_FF ** -0.5),
        'ffn2_post': gain((DEPTH, D_MODEL)),
        'ple_pre': gain((DEPTH, D_MODEL)),
        'w_ple_gate': normal((DEPTH, D_MODEL, D_MODEL), D_MODEL ** -0.5),
        'w_ple_proj': normal((DEPTH, PLE_DIM, D_MODEL), PLE_DIM ** -0.5),
        'ple_post': gain((DEPTH, D_MODEL)),
        'final_norm': gain((D_MODEL,)),
    }


def _stack(states, i):
    return jnp.stack([s[i] for s in states], axis=0)


def reference(x_prompt, x_sample, cache_kv_latent, cache_k_rope, state_ssm, state_conv,
              state_mlstm_c, state_mlstm_n, state_mlstm_m, page_table, p_prompt, p_sample,
              w_in, w_uq, w_uk, w_uv, q_norm, kv_norm, conv_w, conv_b, dt_bias, A_log, ssm_D,
              ssm_norm, b_igate, b_fgate, mlstm_norm, w_branch, w_o, ffn1_pre, ffn1_in, ffn1_out,
              ffn1_post, mix_pre, mix_post, ffn2_pre, ffn2_in, ffn2_out, ffn2_post, ple_pre,
              w_ple_gate, w_ple_proj, ple_post, final_norm):
    dtype = x_prompt.dtype
    Bp, S = x_prompt.shape[:2]
    Bs, Ls = x_sample.shape[:2]
    past_len = page_table.shape[1] * cache_kv_latent.shape[2]
    pos_p = jnp.arange(S)
    pos_s = past_len + jnp.arange(Ls)
    h_p, h_s = x_prompt, x_sample
    new_p, new_s = [], []
    for l in range(DEPTH):
        lw = dict(w_in=w_in[l], w_uq=w_uq[l], w_uk=w_uk[l], w_uv=w_uv[l], q_norm=q_norm[l],
                  kv_norm=kv_norm[l], conv_w=conv_w[l], conv_b=conv_b[l], dt_bias=dt_bias[l],
                  A_log=A_log[l], ssm_D=ssm_D[l], ssm_norm=ssm_norm[l], b_igate=b_igate[l],
                  b_fgate=b_fgate[l], mlstm_norm=mlstm_norm[l], w_branch=w_branch[l], w_o=w_o[l],
                  ffn1_pre=ffn1_pre[l], ffn1_in=ffn1_in[l], ffn1_out=ffn1_out[l], ffn1_post=ffn1_post[l],
                  mix_pre=mix_pre[l], mix_post=mix_post[l], ffn2_pre=ffn2_pre[l], ffn2_in=ffn2_in[l],
                  ffn2_out=ffn2_out[l], ffn2_post=ffn2_post[l], ple_pre=ple_pre[l],
                  w_ple_gate=w_ple_gate[l], w_ple_proj=w_ple_proj[l], ple_post=ple_post[l])
        h_p, st_p = layer(h_p, p_prompt[l], pos_p, None,
                          jnp.zeros((Bp, CONV_W - 1, CONV_DIM), dtype),
                          jnp.zeros((Bp, SSM_HEADS, SSM_HEADDIM, SSM_STATE), dtype),
                          jnp.zeros((Bp, MLSTM_HEADS, MLSTM_QK, MLSTM_V), dtype),
                          jnp.zeros((Bp, MLSTM_HEADS, MLSTM_QK), dtype),
                          jnp.zeros((Bp, MLSTM_HEADS), dtype), lw)
        c_past = cache_kv_latent[l, page_table].reshape(Bs, past_len, KV_LORA)
        kr_past = cache_k_rope[l, page_table].reshape(Bs, past_len, QK_ROPE)
        h_s, st_s = layer(h_s, p_sample[l], pos_s, (c_past, kr_past), state_conv[l], state_ssm[l],
                          state_mlstm_c[l], state_mlstm_n[l], state_mlstm_m[l], lw)
        new_p.append(st_p)
        new_s.append(st_s)
    y_prompt = rmsnorm(h_p, final_norm)
    y_sample = rmsnorm(h_s, final_norm)
    return (y_prompt, y_sample,
            _stack(new_p, 0), _stack(new_p, 1), _stack(new_s, 0), _stack(new_s, 1),
            _stack(new_p, 2), _stack(new_s, 2), _stack(new_p, 3), _stack(new_s, 3),
            _stack(new_p, 4), _stack(new_s, 4), _stack(new_p, 5), _stack(new_s, 5),
            _stack(new_p, 6), _stack(new_s, 6))
```

```python
import functools
import math

import jax
import jax.numpy as jnp
from jax import lax
from jax.experimental import pallas as pl
from jax.experimental.pallas import tpu as pltpu

F32 = jnp.float32
MM_DTYPE = jnp.bfloat16
EPS = 1e-6
NEG_INF = float("-inf")

D_MODEL = 2048
D_FF = 5632
PLE_DIM = 256
MLA_HEADS = 8
Q_LORA = 512
KV_LORA = 512
QK_NOPE = 128
QK_ROPE = 64
V_HEAD = 128
ROPE_THETA = 10000.0
MLA_SCALE = (QK_NOPE + QK_ROPE) ** -0.5
SSM_HEADS = 16
SSM_HEADDIM = 64
SSM_D_INNER = SSM_HEADS * SSM_HEADDIM
SSM_GROUPS = 4
SSM_STATE = 128
CONV_W = 4
CONV_DIM = SSM_D_INNER + 2 * SSM_GROUPS * SSM_STATE
MLSTM_HEADS = 4
MLSTM_QK = 128
MLSTM_V = 256
N_BRANCH = 3
BRANCH_WIDTH = 1024
CHUNK = 128

LANES = 128
SUBLANES = 8
VMEM_LIMIT_BYTES = 56 * 1024 * 1024

PROJ_W = 14336
COL_GATES = 0
COL_XBC = 6144
COL_Z = 8192
COL_MV = 9216
COL_MO = 10240
COL_CQ = 11264
COL_CKV = 11776
COL_MQ = 12288
COL_MK = 12800
COL_KR = 13312
COL_SMALL = 13440
SM_DT = 0
SM_I = 16
SM_F = 20
QCAT = KV_LORA + LANES


def _cparams(sem):
    return pltpu.CompilerParams(dimension_semantics=sem, vmem_limit_bytes=VMEM_LIMIT_BYTES)


def _rms(x, g):
    return x * lax.rsqrt(jnp.mean(x * x, axis=-1, keepdims=True) + EPS) * g


def _mm(a, b):
    return jnp.dot(a.astype(MM_DTYPE), b.astype(MM_DTYPE), preferred_element_type=F32)


def _mm_nt(a, b):
    return lax.dot_general(a.astype(MM_DTYPE), b.astype(MM_DTYPE), (((1,), (1,)), ((), ())),
                           preferred_element_type=F32)


def _mm_tn(a, b):
    return lax.dot_general(a.astype(MM_DTYPE), b.astype(MM_DTYPE), (((0,), (0,)), ((), ())),
                           preferred_element_type=F32)


def _split3(x):
    hi = x.astype(MM_DTYPE)
    r1 = x - hi.astype(F32)
    mid = r1.astype(MM_DTYPE)
    lo = (r1 - mid.astype(F32)).astype(MM_DTYPE)
    return hi, mid, lo


def _mm_exact_rhs(sel, x):
    hi, mid, lo = _split3(x)
    sel = sel.astype(MM_DTYPE)
    dot = functools.partial(jnp.dot, preferred_element_type=F32)
    return dot(sel, hi) + dot(sel, mid) + dot(sel, lo)


def _mm_exact_lhs(x, sel):
    hi, mid, lo = _split3(x)
    sel = sel.astype(MM_DTYPE)
    dot = functools.partial(jnp.dot, preferred_element_type=F32)
    return dot(hi, sel) + dot(mid, sel) + dot(lo, sel)


def _iota(shape, dim):
    return lax.broadcasted_iota(jnp.int32, shape, dim)


def _ffn_body(x_ref, gpre_ref, wa_ref, wb_ref, wo_ref, gpost_ref, o_ref, xn_ref, acc_ref):
    j = pl.program_id(1)

    @pl.when(j == 0)
    def _():
        xn_ref[...] = _rms(x_ref[...], gpre_ref[...]).astype(xn_ref.dtype)
        acc_ref[...] = jnp.zeros_like(acc_ref)

    xn = xn_ref[...]
    a = jnp.dot(xn, wa_ref[...], preferred_element_type=F32)
    b = jnp.dot(xn, wb_ref[...], preferred_element_type=F32)
    hid = (jax.nn.silu(a) * b).astype(xn.dtype)
    acc_ref[...] += jnp.dot(hid, wo_ref[...], preferred_element_type=F32)

    @pl.when(j == pl.num_programs(1) - 1)
    def _():
        o_ref[...] = x_ref[...] + 0.5 * _rms(acc_ref[...], gpost_ref[...])


def ffn(h, g_pre, w_in, w_out, g_post, *, tm, tf):
    t, d = h.shape
    f = w_out.shape[0]
    nf = f // tf
    return pl.pallas_call(
        _ffn_body,
        grid=(t // tm, nf),
        in_specs=[
            pl.BlockSpec((tm, d), lambda i, j: (i, 0)),
            pl.BlockSpec((1, d), lambda i, j: (0, 0)),
            pl.BlockSpec((d, tf), lambda i, j: (0, j)),
            pl.BlockSpec((d, tf), lambda i, j: (0, j + nf)),
            pl.BlockSpec((tf, d), lambda i, j: (j, 0)),
            pl.BlockSpec((1, d), lambda i, j: (0, 0)),
        ],
        out_specs=pl.BlockSpec((tm, d), lambda i, j: (i, 0)),
        out_shape=jax.ShapeDtypeStruct((t, d), F32),
        scratch_shapes=[pltpu.VMEM((tm, d), w_in.dtype), pltpu.VMEM((tm, d), F32)],
        compiler_params=_cparams(("parallel", "arbitrary")),
        name="ffn",
    )(h, g_pre, w_in, w_in, w_out, g_post)


def _norm_matmul_body(x_ref, g_ref, w_ref, o_ref, xn_ref):
    @pl.when(pl.program_id(1) == 0)
    def _():
        xn_ref[...] = _rms(x_ref[...], g_ref[...]).astype(xn_ref.dtype)

    o_ref[...] = jnp.dot(xn_ref[...], w_ref[...], preferred_element_type=F32)


def norm_matmul(h, g, w, *, tm, tn):
    t, d = h.shape
    n = w.shape[1]
    return pl.pallas_call(
        _norm_matmul_body,
        grid=(t // tm, n // tn),
        in_specs=[
            pl.BlockSpec((tm, d), lambda i, j: (i, 0)),
            pl.BlockSpec((1, d), lambda i, j: (0, 0)),
            pl.BlockSpec((d, tn), lambda i, j: (0, j)),
        ],
        out_specs=pl.BlockSpec((tm, tn), lambda i, j: (i, j)),
        out_shape=jax.ShapeDtypeStruct((t, n), F32),
        scratch_shapes=[pltpu.VMEM((tm, d), w.dtype)],
        compiler_params=_cparams(("parallel", "arbitrary")),
        name="mix_in",
    )(h, g, w)


def _merge_body(ya_ref, yb_ref, yc_ref, ga_ref, gb_ref, gc_ref, wa_ref, wb_ref, wc_ref, o_ref):
    acc = jax.nn.sigmoid(ga_ref[...]) * jnp.dot(ya_ref[...], wa_ref[0], preferred_element_type=F32)
    acc += jax.nn.sigmoid(gb_ref[...]) * jnp.dot(yb_ref[...], wb_ref[0], preferred_element_type=F32)
    acc += jax.nn.sigmoid(gc_ref[...]) * jnp.dot(yc_ref[...], wc_ref[0], preferred_element_type=F32)
    o_ref[...] = acc.astype(o_ref.dtype)


def merge_branches(ya, yb, yc, proj, w_branch, *, tm, tn):
    t, bw = ya.shape
    d = w_branch.shape[2]
    nd = d // tn
    yspec = pl.BlockSpec((tm, bw), lambda i, j: (i, 0))

    def gspec(k):
        return pl.BlockSpec((tm, tn), lambda i, j: (i, (COL_GATES // tn) + k * nd + j))

    def wspec(k):
        return pl.BlockSpec((1, bw, tn), lambda i, j: (k, 0, j))

    return pl.pallas_call(
        _merge_body,
        grid=(t // tm, nd),
        in_specs=[yspec, yspec, yspec, gspec(0), gspec(1), gspec(2), wspec(0), wspec(1), wspec(2)],
        out_specs=pl.BlockSpec((tm, tn), lambda i, j: (i, j)),
        out_shape=jax.ShapeDtypeStruct((t, d), w_branch.dtype),
        compiler_params=_cparams(("parallel", "arbitrary")),
        name="merge",
    )(ya, yb, yc, proj, proj, proj, w_branch, w_branch, w_branch)


def _out_proj_body(m_ref, w_ref, g_ref, h_ref, o_ref):
    y = jnp.dot(m_ref[...], w_ref[...], preferred_element_type=F32)
    o_ref[...] = h_ref[...] + _rms(y, g_ref[...])


def out_proj(merged, w_o, g, h, *, tm):
    t, d = h.shape
    return pl.pallas_call(
        _out_proj_body,
        grid=(t // tm,),
        in_specs=[
            pl.BlockSpec((tm, d), lambda i: (i, 0)),
            pl.BlockSpec((d, d), lambda i: (0, 0)),
            pl.BlockSpec((1, d), lambda i: (0, 0)),
            pl.BlockSpec((tm, d), lambda i: (i, 0)),
        ],
        out_specs=pl.BlockSpec((tm, d), lambda i: (i, 0)),
        out_shape=jax.ShapeDtypeStruct((t, d), F32),
        compiler_params=_cparams(("parallel",)),
        name="out_proj",
    )(merged, w_o, g, h)


def _ple_body(h_ref, p_ref, gpre_ref, wg_ref, wp_ref, gpost_ref, gfin_ref, o_ref, y_ref):
    h = h_ref[...]
    dt = wg_ref.dtype
    gate = jax.nn.sigmoid(jnp.dot(_rms(h, gpre_ref[...]).astype(dt), wg_ref[...], preferred_element_type=F32))
    pp = jnp.dot(p_ref[...].astype(dt), wp_ref[...], preferred_element_type=F32)
    hn = h + _rms(gate * pp, gpost_ref[...])
    o_ref[...] = hn
    y_ref[...] = _rms(hn, gfin_ref[...])


def ple(h, p, g_pre, w_gate, w_proj, g_post, g_final, *, tm):
    t, d = h.shape
    pd = p.shape[1]
    row = pl.BlockSpec((tm, d), lambda i: (i, 0))
    vec = pl.BlockSpec((1, d), lambda i: (0, 0))
    return pl.pallas_call(
        _ple_body,
        grid=(t // tm,),
        in_specs=[row, pl.BlockSpec((tm, pd), lambda i: (i, 0)), vec,
                  pl.BlockSpec((d, d), lambda i: (0, 0)), pl.BlockSpec((pd, d), lambda i: (0, 0)), vec, vec],
        out_specs=[row, row],
        out_shape=[jax.ShapeDtypeStruct((t, d), F32), jax.ShapeDtypeStruct((t, d), F32)],
        compiler_params=_cparams(("parallel",)),
        name="ple",
    )(h, p, g_pre, w_gate, w_proj, g_post, g_final)


def _rope128(r, cos, sin):
    first = _iota(r.shape, 1) < QK_ROPE // 2
    partner = jnp.where(first, pltpu.roll(r, LANES - QK_ROPE // 2, 1), pltpu.roll(r, QK_ROPE // 2, 1))
    return r * cos + partner * sin


def _mla_prep_body(cq_ref, ckv_ref, kr_ref, qn_ref, kvn_ref, wuq_ref, wuk_ref, cos_ref, sin_ref,
                   q_out, kcat_out, c_out, kr_out):
    dt = wuq_ref.dtype
    cos, sin = cos_ref[...], sin_ref[...]
    q = jnp.dot(_rms(cq_ref[...], qn_ref[...]).astype(dt), wuq_ref[...], preferred_element_type=F32)
    rope_base = MLA_HEADS * QK_NOPE
    for h in range(MLA_HEADS):
        q_nope = q[:, h * QK_NOPE:(h + 1) * QK_NOPE].astype(dt)
        q_out[:, h * QCAT:h * QCAT + KV_LORA] = jnp.dot(
            q_nope, wuk_ref[h], preferred_element_type=F32).astype(q_out.dtype)
        q_rope = _rope128(q[:, rope_base + h * LANES:rope_base + (h + 1) * LANES], cos, sin)
        q_out[:, h * QCAT + KV_LORA:(h + 1) * QCAT] = q_rope.astype(q_out.dtype)
    c = _rms(ckv_ref[...], kvn_ref[...])
    kr = _rope128(kr_ref[...], cos, sin)
    c_out[...] = c
    kr_out[...] = kr[:, :QK_ROPE]
    kcat_out[:, :KV_LORA] = c.astype(kcat_out.dtype)
    kcat_out[:, KV_LORA:] = kr.astype(kcat_out.dtype)


def mla_prep(proj, q_norm, kv_norm, w_uq, w_uk_t, cos, sin, *, tm):
    t = proj.shape[0]
    return pl.pallas_call(
        _mla_prep_body,
        grid=(t // tm,),
        in_specs=[
            pl.BlockSpec((tm, Q_LORA), lambda i: (i, COL_CQ // Q_LORA)),
            pl.BlockSpec((tm, KV_LORA), lambda i: (i, COL_CKV // KV_LORA)),
            pl.BlockSpec((tm, LANES), lambda i: (i, COL_KR // LANES)),
            pl.BlockSpec((1, Q_LORA), lambda i: (0, 0)),
            pl.BlockSpec((1, KV_LORA), lambda i: (0, 0)),
            pl.BlockSpec(w_uq.shape, lambda i: (0, 0)),
            pl.BlockSpec(w_uk_t.shape, lambda i: (0, 0, 0)),
            pl.BlockSpec((tm, LANES), lambda i: (i, 0)),
            pl.BlockSpec((tm, LANES), lambda i: (i, 0)),
        ],
        out_specs=[
            pl.BlockSpec((tm, MLA_HEADS * QCAT), lambda i: (i, 0)),
            pl.BlockSpec((tm, QCAT), lambda i: (i, 0)),
            pl.BlockSpec((tm, KV_LORA), lambda i: (i, 0)),
            pl.BlockSpec((tm, QK_ROPE), lambda i: (i, 0)),
        ],
        out_shape=[
            jax.ShapeDtypeStruct((t, MLA_HEADS * QCAT), w_uq.dtype),
            jax.ShapeDtypeStruct((t, QCAT), w_uq.dtype),
            jax.ShapeDtypeStruct((t, KV_LORA), F32),
            jax.ShapeDtypeStruct((t, QK_ROPE), F32),
        ],
        compiler_params=_cparams(("parallel",)),
        name="mla_prep",
    )(proj, proj, proj, q_norm, kv_norm, w_uq, w_uk_t, cos, sin)


def _attn_prompt_body(q_ref, k_ref, wuv_ref, o_ref, qall_ref, m_ref, l_ref, acc_ref, *, tq, tk):
    qi, ki = pl.program_id(1), pl.program_id(2)

    @pl.when(ki == 0)
    def _():
        for h in range(MLA_HEADS):
            qall_ref[h * tq:(h + 1) * tq, :] = q_ref[:, h * QCAT:(h + 1) * QCAT]
        m_ref[...] = jnp.full_like(m_ref, NEG_INF)
        l_ref[...] = jnp.zeros_like(l_ref)
        acc_ref[...] = jnp.zeros_like(acc_ref)

    @pl.when(ki * tk <= qi * tq + (tq - 1))
    def _():
        k = k_ref[...]
        s = lax.dot_general(qall_ref[...], k, (((1,), (1,)), ((), ())), preferred_element_type=F32) * MLA_SCALE
        q_pos = qi * tq + (_iota(s.shape, 0) & (tq - 1))
        k_pos = ki * tk + _iota(s.shape, 1)
        s = jnp.where(k_pos <= q_pos, s, NEG_INF)
        m_old = m_ref[...]
        m_new = jnp.maximum(m_old, jnp.max(s, axis=1, keepdims=True))
        alpha = jnp.exp(m_old - m_new)
        p = jnp.exp(s - m_new)
        l_ref[...] = alpha * l_ref[...] + jnp.sum(p, axis=1, keepdims=True)
        acc_ref[...] = alpha * acc_ref[...] + jnp.dot(p.astype(k.dtype), k[:, :KV_LORA],
                                                      preferred_element_type=F32)
        m_ref[...] = m_new

    @pl.when(ki == pl.num_programs(2) - 1)
    def _():
        o = (acc_ref[...] / l_ref[...]).astype(wuv_ref.dtype)
        for h in range(MLA_HEADS):
            o_ref[:, h * V_HEAD:(h + 1) * V_HEAD] = jnp.dot(
                o[h * tq:(h + 1) * tq, :], wuv_ref[h], preferred_element_type=F32).astype(o_ref.dtype)


def attn_prompt(qcat, kcat, w_uv_t, *, batch, seq, tq, tk):
    assert tq & (tq - 1) == 0
    nq, nk = seq // tq, seq // tk

    def kmap(b, qi, ki):
        return (b * nk + jnp.minimum(ki, (qi * tq + tq - 1) // tk), 0)

    return pl.pallas_call(
        functools.partial(_attn_prompt_body, tq=tq, tk=tk),
        grid=(batch, nq, nk),
        in_specs=[
            pl.BlockSpec((tq, MLA_HEADS * QCAT), lambda b, qi, ki: (b * nq + qi, 0)),
            pl.BlockSpec((tk, QCAT), kmap),
            pl.BlockSpec(w_uv_t.shape, lambda b, qi, ki: (0, 0, 0)),
        ],
        out_specs=pl.BlockSpec((tq, MLA_HEADS * V_HEAD), lambda b, qi, ki: (b * nq + qi, 0)),
        out_shape=jax.ShapeDtypeStruct((batch * seq, MLA_HEADS * V_HEAD), w_uv_t.dtype),
        scratch_shapes=[
            pltpu.VMEM((MLA_HEADS * tq, QCAT), qcat.dtype),
            pltpu.VMEM((MLA_HEADS * tq, 1), F32),
            pltpu.VMEM((MLA_HEADS * tq, 1), F32),
            pltpu.VMEM((MLA_HEADS * tq, KV_LORA), F32),
        ],
        compiler_params=_cparams(("parallel", "parallel", "arbitrary")),
        name="attn_prompt",
    )(qcat, kcat, w_uv_t)


def _attn_decode_body(pt_ref, q_ref, knew_ref, *refs, pg):
    del pt_ref
    c_refs, kr_refs = refs[:pg], refs[pg:2 * pg]
    o_ref, m_ref, l_ref, acc_ref = refs[2 * pg:]
    j = pl.program_id(1)

    @pl.when(j == 0)
    def _():
        m_ref[...] = jnp.full_like(m_ref, NEG_INF)
        l_ref[...] = jnp.zeros_like(l_ref)
        acc_ref[...] = jnp.zeros_like(acc_ref)

    q = q_ref[0]
    ql = q[:, :KV_LORA].astype(MM_DTYPE)
    qr = q[:, KV_LORA:KV_LORA + QK_ROPE].astype(MM_DTYPE)
    cs = [c_refs[g][...].astype(MM_DTYPE) for g in range(pg)]
    s = jnp.concatenate(
        [_mm_nt(ql, cs[g]) + _mm_nt(qr, kr_refs[g][...]) for g in range(pg)], axis=1) * MLA_SCALE
    m_old = m_ref[...]
    m_new = jnp.maximum(m_old, jnp.max(s, axis=1, keepdims=True))
    alpha = jnp.exp(m_old - m_new)
    p = jnp.exp(s - m_new)
    l_new = alpha * l_ref[...] + jnp.sum(p, axis=1, keepdims=True)
    pv = _mm(p[:, :CHUNK], cs[0])
    for g in range(1, pg):
        pv += _mm(p[:, g * CHUNK:(g + 1) * CHUNK], cs[g])
    acc_new = alpha * acc_ref[...] + pv
    m_ref[...] = m_new
    l_ref[...] = l_new
    acc_ref[...] = acc_new

    @pl.when(j == pl.num_programs(1) - 1)
    def _():
        knew = knew_ref[0]
        s_new = jnp.sum(q * knew, axis=1, keepdims=True) * MLA_SCALE
        m_fin = jnp.maximum(m_new, s_new)
        a2 = jnp.exp(m_new - m_fin)
        p_new = jnp.exp(s_new - m_fin)
        l_fin = a2 * l_new + p_new
        acc_fin = a2 * acc_new + p_new.astype(MM_DTYPE).astype(F32) * knew[:, :KV_LORA]
        o_ref[0] = acc_fin / l_fin


def attn_decode(page_table, q_s, k_new, cache_c, cache_kr, layer, *, pg):
    bsz, n_pages = page_table.shape
    nch = n_pages // pg

    def cmap(g):
        return lambda b, j, pt: (layer, pt[b, j * pg + g], 0, 0)

    c_specs = [pl.BlockSpec((None, None, CHUNK, KV_LORA), cmap(g)) for g in range(pg)]
    kr_specs = [pl.BlockSpec((None, None, CHUNK, QK_ROPE), cmap(g)) for g in range(pg)]
    grid_spec = pltpu.PrefetchScalarGridSpec(
        num_scalar_prefetch=1,
        grid=(bsz, nch),
        in_specs=[pl.BlockSpec((1, MLA_HEADS, QCAT), lambda b, j, pt: (b, 0, 0)),
                  pl.BlockSpec((1, 1, QCAT), lambda b, j, pt: (b, 0, 0))] + c_specs + kr_specs,
        out_specs=pl.BlockSpec((1, MLA_HEADS, KV_LORA), lambda b, j, pt: (b, 0, 0)),
        scratch_shapes=[pltpu.VMEM((MLA_HEADS, 1), F32), pltpu.VMEM((MLA_HEADS, 1), F32),
                        pltpu.VMEM((MLA_HEADS, KV_LORA), F32)],
    )
    return pl.pallas_call(
        functools.partial(_attn_decode_body, pg=pg),
        grid_spec=grid_spec,
        out_shape=jax.ShapeDtypeStruct((bsz, MLA_HEADS, KV_LORA), F32),
        compiler_params=_cparams(("parallel", "arbitrary")),
        name="attn_decode",
    )(page_table, q_s, k_new, *([cache_c] * pg), *([cache_kr] * pg))


def _uv_proj_body(o_ref, wuv_ref, y_ref):
    for h in range(MLA_HEADS):
        y_ref[:, h * V_HEAD:(h + 1) * V_HEAD] = jnp.dot(
            o_ref[:, h * KV_LORA:(h + 1) * KV_LORA].astype(wuv_ref.dtype), wuv_ref[h],
            preferred_element_type=F32).astype(y_ref.dtype)


def uv_proj(o_lat, w_uv_t):
    bsz = o_lat.shape[0]
    return pl.pallas_call(
        _uv_proj_body,
        out_shape=jax.ShapeDtypeStruct((bsz, MLA_HEADS * V_HEAD), w_uv_t.dtype),
        compiler_params=pltpu.CompilerParams(vmem_limit_bytes=VMEM_LIMIT_BYTES),
        name="uv_proj",
    )(o_lat, w_uv_t)


def _head_expand_matrix(n_rows, width):
    return (_iota((n_rows, SSM_HEADS * width), 0) == _iota((n_rows, SSM_HEADS * width), 1) // width)


def _ssd_finish(y, xs, z, d_row, norm_row):
    y = (y + xs * d_row) * jax.nn.silu(z)
    gw = SSM_D_INNER // SSM_GROUPS
    parts = []
    for g in range(SSM_GROUPS):
        parts.append(_rms(y[:, g * gw:(g + 1) * gw], norm_row[:, g * gw:(g + 1) * gw]))
    return jnp.concatenate(parts, axis=1)


def _ssd_prompt_body(xbc_ref, z_ref, sm_ref, cw_ref, cb_ref, dtb_ref, alog_ref, d_ref, norm_ref,
                     y_ref, s_out_ref, s_ref, xp_ref, ybuf_ref):
    c = pl.program_id(1)
    tail = SUBLANES

    @pl.when(c == 0)
    def _():
        s_ref[...] = jnp.zeros_like(s_ref)
        xp_ref[0:tail, :] = jnp.zeros((tail, CONV_DIM), F32)

    xp_ref[tail:tail + CHUNK, :] = xbc_ref[...]
    cw = cw_ref[...]
    conv = xp_ref[tail - 3:tail - 3 + CHUNK, :] * cw[0:1, :]
    for k in range(1, CONV_W):
        conv = conv + xp_ref[tail - 3 + k:tail - 3 + k + CHUNK, :] * cw[k:k + 1, :]
    xc = jax.nn.silu(conv + cb_ref[...])
    xp_ref[0:tail, :] = xp_ref[CHUNK:CHUNK + tail, :]
    xs = xc[:, :SSM_D_INNER]
    bm = xc[:, SSM_D_INNER:SSM_D_INNER + SSM_GROUPS * SSM_STATE].astype(MM_DTYPE)
    cm = xc[:, SSM_D_INNER + SSM_GROUPS * SSM_STATE:].astype(MM_DTYPE)

    dt = jax.nn.softplus(sm_ref[...] + dtb_ref[...])
    a = dt * (-jnp.exp(alog_ref[...]))
    row, col = _iota((CHUNK, CHUNK), 0), _iota((CHUNK, CHUNK), 1)
    causal = row >= col
    a_cum = _mm_exact_rhs(causal, a)
    a_cum_t = a_cum.T
    a_last = a_cum[CHUNK - 1:CHUNK, :]
    expand = _head_expand_matrix(LANES, SSM_HEADDIM)
    dt_e = _mm_exact_lhs(dt, expand)
    ea_e = jnp.exp(_mm_exact_lhs(a_cum, expand))
    wst_e = _mm_exact_lhs(jnp.exp(a_last - a_cum) * dt, expand)
    xdt = xs * dt_e
    xw = xs * wst_e
    pair_lane_lo = _iota((CHUNK, LANES), 1) < SSM_HEADDIM
    pair_row_lo = _iota((LANES, SSM_STATE), 0) < SSM_HEADDIM
    heads_per_group = SSM_HEADS // SSM_GROUPS

    for g in range(SSM_GROUPS):
        bg = bm[:, g * SSM_STATE:(g + 1) * SSM_STATE]
        cg = cm[:, g * SSM_STATE:(g + 1) * SSM_STATE]
        cb = lax.dot_general(cg, bg, (((1,), (1,)), ((), ())), preferred_element_type=F32)
        for jp in range(heads_per_group // 2):
            pair = g * (heads_per_group // 2) + jp
            lanes = slice(pair * LANES, (pair + 1) * LANES)
            ms = []
            for h in (2 * pair, 2 * pair + 1):
                seg = a_cum[:, h:h + 1] - a_cum_t[h:h + 1, :]
                ms.append((cb * jnp.exp(jnp.where(causal, seg, NEG_INF))).astype(MM_DTYPE))
            m2 = jnp.concatenate(ms, axis=1)
            xp = xdt[:, lanes]
            x2 = jnp.concatenate([jnp.where(pair_lane_lo, xp, 0.0), jnp.where(pair_lane_lo, 0.0, xp)],
                                 axis=0).astype(MM_DTYPE)
            y_diag = jnp.dot(m2, x2, preferred_element_type=F32)
            s_prev = s_ref[pair * LANES:(pair + 1) * LANES, :]
            y_off = _mm_nt(cg, s_prev) * ea_e[:, lanes]
            ybuf_ref[:, lanes] = y_diag + y_off
            st = _mm_tn(xw[:, lanes], bg)
            dec = jnp.exp(jnp.where(pair_row_lo, a_cum_t[2 * pair:2 * pair + 1, CHUNK - 1:CHUNK],
                                    a_cum_t[2 * pair + 1:2 * pair + 2, CHUNK - 1:CHUNK]))
            s_ref[pair * LANES:(pair + 1) * LANES, :] = s_prev * dec + st

    y_ref[...] = _ssd_finish(ybuf_ref[...], xs, z_ref[...], d_ref[...], norm_ref[...]).astype(y_ref.dtype)

    @pl.when(c == pl.num_programs(1) - 1)
    def _():
        s_out_ref[0] = s_ref[...]


def _ssd_param_specs():
    def full(shape):
        return pl.BlockSpec(shape, lambda *_: (0,) * len(shape))

    return [full((CONV_W, CONV_DIM)), full((1, CONV_DIM)), full((1, LANES)), full((1, LANES)),
            full((1, SSM_D_INNER)), full((1, SSM_D_INNER))]


def ssd_prompt(proj, conv_w, conv_b, dt_bias, a_log, d_e, norm, *, batch, seq, out_dtype):
    nc = seq // CHUNK

    def rows(width, col):
        return pl.BlockSpec((CHUNK, width), lambda b, c: (b * nc + c, col // width))

    return pl.pallas_call(
        _ssd_prompt_body,
        grid=(batch, nc),
        in_specs=[rows(CONV_DIM, COL_XBC), rows(SSM_D_INNER, COL_Z), rows(LANES, COL_SMALL)] + _ssd_param_specs(),
        out_specs=[pl.BlockSpec((CHUNK, SSM_D_INNER), lambda b, c: (b * nc + c, 0)),
                   pl.BlockSpec((1, SSM_D_INNER, SSM_STATE), lambda b, c: (b, 0, 0))],
        out_shape=[jax.ShapeDtypeStruct((batch * seq, SSM_D_INNER), out_dtype),
                   jax.ShapeDtypeStruct((batch, SSM_D_INNER, SSM_STATE), F32)],
        scratch_shapes=[pltpu.VMEM((SSM_D_INNER, SSM_STATE), F32),
                        pltpu.VMEM((CHUNK + SUBLANES, CONV_DIM), F32),
                        pltpu.VMEM((CHUNK, SSM_D_INNER), F32)],
        compiler_params=_cparams(("parallel", "arbitrary")),
        name="ssd_prompt",
    )(proj, proj, proj, conv_w, conv_b, dt_bias, a_log, d_e, norm)


def _round_mm(x):
    return x.astype(MM_DTYPE).astype(F32)


def _hi_lo(x):
    hi = _round_mm(x)
    return hi, _round_mm(x - hi)


def _ssd_sample_body(xbc_ref, z_ref, sm_ref, cs_ref, s0_ref, cw_ref, cb_ref, dtb_ref, alog_ref, d_ref, norm_ref,
                     y_ref, s_out_ref, *, bb):
    cw = cw_ref[...]
    conv = cs_ref[0] * cw[0:1, :]
    for k in range(1, CONV_W - 1):
        conv = conv + cs_ref[k] * cw[k:k + 1, :]
    conv = conv + xbc_ref[...] * cw[CONV_W - 1:CONV_W, :]
    xc = jax.nn.silu(conv + cb_ref[...])
    xs = xc[:, :SSM_D_INNER]
    bm = xc[:, SSM_D_INNER:SSM_D_INNER + SSM_GROUPS * SSM_STATE]
    cm = xc[:, SSM_D_INNER + SSM_GROUPS * SSM_STATE:]
    dt = jax.nn.softplus(sm_ref[...] + dtb_ref[...])
    dec = jnp.exp(dt * (-jnp.exp(alog_ref[...])))
    expand = _head_expand_matrix(LANES, SSM_HEADDIM)
    dt_e = _mm_exact_lhs(dt, expand)
    dec_e = _mm_exact_lhs(dec, expand)
    dtx = dt_e * xs
    gw = SSM_D_INNER // SSM_GROUPS
    lane_group = _iota((1, SSM_D_INNER), 1) // gw

    cb_e = jnp.zeros_like(xs)
    for g in range(SSM_GROUPS):
        sl = slice(g * SSM_STATE, (g + 1) * SSM_STATE)
        cb_g = jnp.sum(cm[:, sl] * bm[:, sl], axis=1, keepdims=True)
        cb_e = jnp.where(lane_group == g, cb_g, cb_e)

    kk = 4 * SSM_GROUPS
    r_l = _iota((kk, SSM_D_INNER), 0)
    r_b = _iota((kk, SSM_STATE), 0)
    group_match = (r_l % SSM_GROUPS) == (_iota((kk, SSM_D_INNER), 1) // gw)
    ones_rows = jnp.where(_iota((SUBLANES, SSM_STATE), 0) < 3, 1.0, 0.0).astype(MM_DTYPE)
    r_d = _iota((SUBLANES, SSM_D_INNER), 0)
    y_off_rows = []
    for i in range(bb):
        s0 = s0_ref[i]
        x_hi, x_lo = _hi_lo(dtx[i:i + 1, :])
        lhs = jnp.where(group_match,
                        jnp.where(r_l < 2 * SSM_GROUPS, x_hi, jnp.where(r_l < 3 * SSM_GROUPS, x_lo, 0.0)), 0.0)
        b_sel = jnp.zeros((kk, SSM_STATE), F32)
        c_sel = jnp.zeros((SUBLANES, SSM_STATE), F32)
        for g in range(SSM_GROUPS):
            sl = slice(g * SSM_STATE, (g + 1) * SSM_STATE)
            b_sel = jnp.where(r_b % SSM_GROUPS == g, bm[i:i + 1, sl], b_sel)
            c_sel = jnp.where(_iota((SUBLANES, SSM_STATE), 0) == g, cm[i:i + 1, sl], c_sel)
        b_hi, b_lo = _hi_lo(b_sel)
        in_mid = (r_b >= SSM_GROUPS) & (r_b < 2 * SSM_GROUPS)
        rhs = jnp.where(in_mid, b_lo, jnp.where(r_b < 3 * SSM_GROUPS, b_hi, 0.0))
        st = _mm_tn(lhs, rhs)
        d_row = dec_e[i:i + 1, :]
        d_hi, d_mid = _hi_lo(d_row)
        d_lo = _round_mm(d_row - d_hi - d_mid)
        d_lhs = jnp.where(r_d == 0, d_hi, jnp.where(r_d == 1, d_mid, jnp.where(r_d == 2, d_lo, 0.0)))
        dec_full = _mm_tn(d_lhs, ones_rows)
        s_out_ref[i] = s0 * dec_full + st
        cs0 = _mm_nt(c_sel, s0)
        y_off_rows.append(jnp.sum(jnp.where(r_d == lane_group, cs0, 0.0), axis=0, keepdims=True))
    y_off = jnp.concatenate(y_off_rows, axis=0) * dec_e
    y = cb_e * dtx + y_off
    y_ref[...] = _ssd_finish(y, xs, z_ref[...], d_ref[...], norm_ref[...]).astype(y_ref.dtype)


def ssd_sample(proj, conv_state_t, s0, conv_w, conv_b, dt_bias, a_log, d_e, norm, *, row0, bsz, bb, out_dtype):
    base = row0 // bb

    def rows(width, col):
        return pl.BlockSpec((bb, width), lambda i: (base + i, col // width))

    return pl.pallas_call(
        functools.partial(_ssd_sample_body, bb=bb),
        grid=(bsz // bb,),
        in_specs=[rows(CONV_DIM, COL_XBC), rows(SSM_D_INNER, COL_Z), rows(LANES, COL_SMALL),
                  pl.BlockSpec((CONV_W - 1, bb, CONV_DIM), lambda i: (0, i, 0)),
                  pl.BlockSpec((bb, SSM_D_INNER, SSM_STATE), lambda i: (i, 0, 0))] + _ssd_param_specs(),
        out_specs=[pl.BlockSpec((bb, SSM_D_INNER), lambda i: (i, 0)),
                   pl.BlockSpec((bb, SSM_D_INNER, SSM_STATE), lambda i: (i, 0, 0))],
        out_shape=[jax.ShapeDtypeStruct((bsz, SSM_D_INNER), out_dtype),
                   jax.ShapeDtypeStruct((bsz, SSM_D_INNER, SSM_STATE), F32)],
        compiler_params=_cparams(("parallel",)),
        name="ssd_sample",
    )(proj, proj, proj, conv_state_t, s0, conv_w, conv_b, dt_bias, a_log, d_e, norm)


def _mlstm_prompt_body(qk_ref, v_ref, o_ref, sm_ref, bi_ref, bf_ref, norm_ref,
                       y_ref, c_out_ref, n_out_ref, m_out_ref, c_ref, n_ref, m_ref):
    ci = pl.program_id(1)

    @pl.when(ci == 0)
    def _():
        c_ref[...] = jnp.zeros_like(c_ref)
        n_ref[...] = jnp.zeros_like(n_ref)
        m_ref[...] = jnp.zeros_like(m_ref)

    sm = sm_ref[...]
    i_pre = sm + bi_ref[...]
    logf = jax.nn.log_sigmoid(sm + bf_ref[...])
    row, col = _iota((CHUNK, CHUNK), 0), _iota((CHUNK, CHUNK), 1)
    causal = row >= col
    b_cum = _mm_exact_rhs(causal, logf)
    b_cum_t = b_cum.T
    i_pre_t = i_pre.T
    qscale = MLSTM_QK ** -0.5
    hq = MLSTM_HEADS * MLSTM_QK
    for h in range(MLSTM_HEADS):
        b_col = b_cum[:, SM_F + h:SM_F + h + 1]
        i_col = i_pre[:, SM_I + h:SM_I + h + 1]
        b_row = b_cum_t[SM_F + h:SM_F + h + 1, :]
        i_row = i_pre_t[SM_I + h:SM_I + h + 1, :]
        m_prev = m_ref[h:h + 1, 0:1]
        dmat = jnp.where(causal, b_col - b_row + i_row, NEG_INF)
        inter = b_col + m_prev
        m_row = jnp.maximum(jnp.max(dmat, axis=1, keepdims=True), inter)
        w = jnp.exp(dmat - m_row)
        w_inter = jnp.exp(inter - m_row)
        q = qk_ref[:, h * MLSTM_QK:(h + 1) * MLSTM_QK] * qscale
        k = qk_ref[:, hq + h * MLSTM_QK:hq + (h + 1) * MLSTM_QK]
        v = v_ref[:, h * MLSTM_V:(h + 1) * MLSTM_V]
        qk = _mm_nt(q, k) * w
        c_prev = c_ref[h * MLSTM_QK:(h + 1) * MLSTM_QK, :]
        n_prev = n_ref[h:h + 1, :]
        num = _mm(qk, v) + w_inter * _mm(q, c_prev)
        den = jnp.sum(qk, axis=1, keepdims=True) + w_inter * jnp.sum(q * n_prev, axis=1, keepdims=True)
        hh = num / jnp.maximum(jnp.abs(den), jnp.exp(-m_row))
        b_last = b_col[CHUNK - 1:CHUNK, :]
        gcol = b_last - b_col + i_col
        m_new = jnp.maximum(b_last + m_prev, jnp.max(gcol, axis=0, keepdims=True))
        wk = jnp.exp(gcol - m_new)
        wc = jnp.exp(b_last + m_prev - m_new)
        kw = k * wk
        c_ref[h * MLSTM_QK:(h + 1) * MLSTM_QK, :] = wc * c_prev + _mm_tn(kw, v)
        n_ref[h:h + 1, :] = wc * n_prev + jnp.sum(kw, axis=0, keepdims=True)
        m_ref[h:h + 1, :] = jnp.broadcast_to(m_new, (1, LANES))
        vs = slice(h * MLSTM_V, (h + 1) * MLSTM_V)
        y_ref[:, vs] = (jax.nn.sigmoid(o_ref[:, vs]) * _rms(hh, norm_ref[:, vs])).astype(y_ref.dtype)

    @pl.when(ci == pl.num_programs(1) - 1)
    def _():
        c_out_ref[0] = c_ref[...]
        n_out_ref[0] = n_ref[...]
        m_out_ref[0] = m_ref[...]


def mlstm_prompt(proj, b_i, b_f, norm, *, batch, seq, out_dtype):
    nc = seq // CHUNK
    hq = MLSTM_HEADS * MLSTM_QK

    def rows(width, col):
        return pl.BlockSpec((CHUNK, width), lambda b, c: (b * nc + c, col // width))

    def vec(width):
        return pl.BlockSpec((1, width), lambda b, c: (0, 0))

    return pl.pallas_call(
        _mlstm_prompt_body,
        grid=(batch, nc),
        in_specs=[rows(2 * hq, COL_MQ), rows(BRANCH_WIDTH, COL_MV), rows(BRANCH_WIDTH, COL_MO), rows(LANES, COL_SMALL),
                  vec(LANES), vec(LANES), vec(BRANCH_WIDTH)],
        out_specs=[pl.BlockSpec((CHUNK, BRANCH_WIDTH), lambda b, c: (b * nc + c, 0)),
                   pl.BlockSpec((1, hq, MLSTM_V), lambda b, c: (b, 0, 0)),
                   pl.BlockSpec((1, SUBLANES, MLSTM_QK), lambda b, c: (b, 0, 0)),
                   pl.BlockSpec((1, SUBLANES, LANES), lambda b, c: (b, 0, 0))],
        out_shape=[jax.ShapeDtypeStruct((batch * seq, BRANCH_WIDTH), out_dtype),
                   jax.ShapeDtypeStruct((batch, hq, MLSTM_V), F32),
                   jax.ShapeDtypeStruct((batch, SUBLANES, MLSTM_QK), F32),
                   jax.ShapeDtypeStruct((batch, SUBLANES, LANES), F32)],
        scratch_shapes=[pltpu.VMEM((hq, MLSTM_V), F32), pltpu.VMEM((SUBLANES, MLSTM_QK), F32),
                        pltpu.VMEM((SUBLANES, LANES), F32)],
        compiler_params=_cparams(("parallel", "arbitrary")),
        name="mlstm_prompt",
    )(proj, proj, proj, proj, b_i, b_f, norm)


def _mlstm_sample_body(qk_ref, v_ref, o_ref, sm_ref, c0_ref, n0_ref, m0_ref, bi_ref, bf_ref, norm_ref,
                       y_ref, c_out_ref, n_out_ref, m_out_ref, *, bb):
    sm = sm_ref[...]
    i_pre = sm + bi_ref[...]
    logf = jax.nn.log_sigmoid(sm + bf_ref[...])
    qscale = MLSTM_QK ** -0.5
    hq = MLSTM_HEADS * MLSTM_QK
    r8k = _iota((SUBLANES, MLSTM_QK), 0)
    r8v = _iota((SUBLANES, MLSTM_V), 0)
    m_out = jnp.zeros((bb, LANES), F32)
    lane = _iota((bb, LANES), 1)
    for h in range(MLSTM_HEADS):
        ib = i_pre[:, SM_I + h:SM_I + h + 1]
        lf = logf[:, SM_F + h:SM_F + h + 1]
        m_prev = m0_ref[:, h:h + 1]
        inter = lf + m_prev
        m_new = jnp.maximum(ib, inter)
        w = jnp.exp(ib - m_new)
        w_inter = jnp.exp(inter - m_new)
        q = qk_ref[:, h * MLSTM_QK:(h + 1) * MLSTM_QK] * qscale
        k = qk_ref[:, hq + h * MLSTM_QK:hq + (h + 1) * MLSTM_QK]
        v = v_ref[:, h * MLSTM_V:(h + 1) * MLSTM_V]
        n_prev = n0_ref[h]
        qk = jnp.sum(q * k, axis=1, keepdims=True) * w
        kw = k * w
        qc_rows = []
        for i in range(bb):
            c_prev = c0_ref[i, h * MLSTM_QK:(h + 1) * MLSTM_QK, :]
            qc_rows.append(_mm(q[i:i + 1, :], c_prev))
            k_hi, k_lo = _hi_lo(kw[i:i + 1, :])
            v_hi, v_lo = _hi_lo(v[i:i + 1, :])
            lhs = jnp.where(r8k < 2, k_hi, jnp.where(r8k == 2, k_lo, 0.0))
            rhs = jnp.where(r8v == 0, v_hi, jnp.where(r8v == 1, v_lo, jnp.where(r8v == 2, v_hi, 0.0)))
            outer = _mm_tn(lhs, rhs)
            c_out_ref[i, h * MLSTM_QK:(h + 1) * MLSTM_QK, :] = w_inter[i:i + 1, :] * c_prev + outer
        qc = jnp.concatenate(qc_rows, axis=0)
        num = qk * v + w_inter * qc
        den = qk + w_inter * jnp.sum(q * n_prev, axis=1, keepdims=True)
        hh = num / jnp.maximum(jnp.abs(den), jnp.exp(-m_new))
        n_out_ref[h] = w_inter * n_prev + kw
        m_out = jnp.where(lane == h, m_new, m_out)
        vs = slice(h * MLSTM_V, (h + 1) * MLSTM_V)
        y_ref[:, vs] = (jax.nn.sigmoid(o_ref[:, vs]) * _rms(hh, norm_ref[:, vs])).astype(y_ref.dtype)
    m_out_ref[...] = m_out


def mlstm_sample(proj, c0, n0_t, m0, b_i, b_f, norm, *, row0, bsz, bb, out_dtype):
    base = row0 // bb
    hq = MLSTM_HEADS * MLSTM_QK

    def rows(width, col):
        return pl.BlockSpec((bb, width), lambda i: (base + i, col // width))

    def vec(width):
        return pl.BlockSpec((1, width), lambda i: (0, 0))

    cspec = pl.BlockSpec((bb, hq, MLSTM_V), lambda i: (i, 0, 0))
    nspec = pl.BlockSpec((MLSTM_HEADS, bb, MLSTM_QK), lambda i: (0, i, 0))
    return pl.pallas_call(
        functools.partial(_mlstm_sample_body, bb=bb),
        grid=(bsz // bb,),
        in_specs=[rows(2 * hq, COL_MQ), rows(BRANCH_WIDTH, COL_MV), rows(BRANCH_WIDTH, COL_MO), rows(LANES, COL_SMALL),
                  cspec, nspec, pl.BlockSpec((bb, MLSTM_HEADS), lambda i: (i, 0)),
                  vec(LANES), vec(LANES), vec(BRANCH_WIDTH)],
        out_specs=[pl.BlockSpec((bb, BRANCH_WIDTH), lambda i: (i, 0)), cspec, nspec,
                   pl.BlockSpec((bb, LANES), lambda i: (i, 0))],
        out_shape=[jax.ShapeDtypeStruct((bsz, BRANCH_WIDTH), out_dtype),
                   jax.ShapeDtypeStruct((bsz, hq, MLSTM_V), F32),
                   jax.ShapeDtypeStruct((MLSTM_HEADS, bsz, MLSTM_QK), F32),
                   jax.ShapeDtypeStruct((bsz, LANES), F32)],
        compiler_params=_cparams(("parallel",)),
        name="mlstm_sample",
    )(proj, proj, proj, proj, c0, n0_t, m0, b_i, b_f, norm)


def _prep_w_in(w):
    sizes = (Q_LORA, KV_LORA, QK_ROPE, SSM_D_INNER, CONV_DIM, SSM_HEADS,
             MLSTM_HEADS * MLSTM_QK, MLSTM_HEADS * MLSTM_QK, MLSTM_HEADS * MLSTM_V,
             MLSTM_HEADS, MLSTM_HEADS, MLSTM_HEADS * MLSTM_V, N_BRANCH * D_MODEL)
    parts, start = [], 0
    for s in sizes:
        parts.append(w[:, start:start + s])
        start += s
    cq, ckv, kr, z, xbc, dt, mq, mk, mv, mi, mf, mo, gates = parts
    d = w.shape[0]

    def zeros(n):
        return jnp.zeros((d, n), w.dtype)

    out = jnp.concatenate(
        [gates, xbc, z, mv, mo, cq, ckv, mq, mk,
         kr, zeros(LANES - QK_ROPE),
         dt, mi, mf, zeros(LANES - SSM_HEADS - 2 * MLSTM_HEADS),
         zeros(PROJ_W - COL_SMALL - LANES)], axis=1)
    return out.astype(MM_DTYPE)


def _prep_w_uq(w):
    w3 = w.reshape(Q_LORA, MLA_HEADS, QK_NOPE + QK_ROPE)
    nope = w3[:, :, :QK_NOPE].reshape(Q_LORA, MLA_HEADS * QK_NOPE)
    rope = jnp.pad(w3[:, :, QK_NOPE:], ((0, 0), (0, 0), (0, LANES - QK_ROPE))).reshape(Q_LORA, MLA_HEADS * LANES)
    return jnp.concatenate([nope, rope], axis=1).astype(MM_DTYPE)


def _lane_pad(v, offset):
    return jnp.zeros((1, LANES), F32).at[0, offset:offset + v.shape[0]].set(v)


def _rope_tables(pos):
    half = QK_ROPE // 2
    inv = ROPE_THETA ** (-jnp.arange(half, dtype=F32) / half)
    ang = pos.astype(F32)[:, None] * inv[None, :]
    cos, sin = jnp.cos(ang), jnp.sin(ang)
    zeros = jnp.zeros((pos.shape[0], LANES - QK_ROPE), F32)
    return jnp.concatenate([cos, cos, zeros], axis=1), jnp.concatenate([-sin, sin, zeros], axis=1)


def _tiles(t):
    for tm in range(min(640, t), 15, -16):
        if t % tm == 0:
            return tm
    raise ValueError(f"no row tile for {t} rows")


def kernel(x_prompt, x_sample, cache_kv_latent, cache_k_rope, state_ssm, state_conv, state_mlstm_c, state_mlstm_n, state_mlstm_m, page_table, p_prompt, p_sample, w_in, w_uq, w_uk, w_uv, q_norm, kv_norm, conv_w, conv_b, dt_bias, A_log, ssm_D, ssm_norm, b_igate, b_fgate, mlstm_norm, w_branch, w_o, ffn1_pre, ffn1_in, ffn1_out, ffn1_post, mix_pre, mix_post, ffn2_pre, ffn2_in, ffn2_out, ffn2_post, ple_pre, w_ple_gate, w_ple_proj, ple_post, final_norm):
    bp, seq, d = x_prompt.shape
    bs = x_sample.shape[0]
    depth = w_in.shape[0]
    tp = bp * seq
    t = tp + bs
    n_pages = page_table.shape[1]
    past_len = n_pages * cache_kv_latent.shape[2]
    tm = _tiles(t)
    bb = SUBLANES
    act = MM_DTYPE

    h = jnp.concatenate([x_prompt.reshape(tp, d), x_sample.reshape(bs, d)], axis=0)
    pos = jnp.concatenate([jnp.tile(jnp.arange(seq), bp), jnp.full((bs,), past_len)])
    cos, sin = _rope_tables(pos)

    def row(v):
        return v.reshape(1, -1).astype(F32)

    outs = {k: [] for k in ("c_p", "kr_p", "c_s", "kr_s", "ssm_p", "ssm_s", "conv_p", "conv_s",
                            "mc_p", "mc_s", "mn_p", "mn_s", "mm_p", "mm_s")}
    y_final = None
    for l in range(depth):
        mm = lambda w: w.astype(MM_DTYPE)
        h = ffn(h, row(ffn1_pre[l]), mm(ffn1_in[l]), mm(ffn1_out[l]), row(ffn1_post[l]), tm=tm, tf=512)
        proj = norm_matmul(h, row(mix_pre[l]), _prep_w_in(w_in[l]), tm=tm, tn=1024)

        qcat, kcat, c_lat, k_rope = mla_prep(
            proj, row(q_norm[l]), row(kv_norm[l]), _prep_w_uq(w_uq[l]),
            mm(jnp.transpose(w_uk[l], (1, 2, 0))), cos, sin, tm=tm)
        w_uv_t = mm(jnp.transpose(w_uv[l], (1, 0, 2)))
        ya_p = attn_prompt(qcat, kcat, w_uv_t, batch=bp, seq=seq, tq=min(128, seq), tk=min(512, seq))
        q_s = qcat[tp:].reshape(bs, MLA_HEADS, QCAT).astype(F32)
        k_s = kcat[tp:].reshape(bs, 1, QCAT).astype(F32)
        o_s = attn_decode(page_table, q_s, k_s, cache_kv_latent, cache_k_rope, l, pg=min(8, n_pages))
        ya_s = uv_proj(o_s.reshape(bs, MLA_HEADS * KV_LORA), w_uv_t)

        ssd_params = (conv_w[l], row(conv_b[l]), _lane_pad(dt_bias[l], SM_DT), _lane_pad(A_log[l], SM_DT),
                      row(jnp.repeat(ssm_D[l], SSM_HEADDIM)), row(ssm_norm[l]))
        yb_p, ssm_p = ssd_prompt(proj, *ssd_params, batch=bp, seq=seq, out_dtype=act)
        yb_s, ssm_s = ssd_sample(proj, jnp.swapaxes(state_conv[l], 0, 1),
                                 state_ssm[l].reshape(bs, SSM_D_INNER, SSM_STATE), *ssd_params,
                                 row0=tp, bsz=bs, bb=bb, out_dtype=act)
        conv_p = proj[:tp].reshape(bp, seq, PROJ_W)[:, seq - (CONV_W - 1):, COL_XBC:COL_XBC + CONV_DIM]
        xbc_s = proj[tp:, COL_XBC:COL_XBC + CONV_DIM]

        ml_params = (_lane_pad(b_igate[l], SM_I), _lane_pad(b_fgate[l], SM_F), row(mlstm_norm[l]))
        yc_p, mc_p, mn_p, mm_p = mlstm_prompt(proj, *ml_params, batch=bp, seq=seq, out_dtype=act)
        yc_s, mc_s, mn_s, mm_s = mlstm_sample(
            proj, state_mlstm_c[l].reshape(bs, MLSTM_HEADS * MLSTM_QK, MLSTM_V),
            jnp.swapaxes(state_mlstm_n[l], 0, 1), state_mlstm_m[l], *ml_params,
            row0=tp, bsz=bs, bb=bb, out_dtype=act)

        ya = jnp.concatenate([ya_p, ya_s], axis=0)
        yb = jnp.concatenate([yb_p, yb_s], axis=0)
        yc = jnp.concatenate([yc_p, yc_s], axis=0)
        merged = merge_branches(ya, yb, yc, proj, mm(w_branch[l]), tm=tm, tn=1024)
        h = out_proj(merged, mm(w_o[l]), row(mix_post[l]), h, tm=tm)
        h = ffn(h, row(ffn2_pre[l]), mm(ffn2_in[l]), mm(ffn2_out[l]), row(ffn2_post[l]), tm=tm, tf=512)
        p = jnp.concatenate([p_prompt[l].reshape(tp, PLE_DIM), p_sample[l].reshape(bs, PLE_DIM)], axis=0)
        h, y_final = ple(h, p, row(ple_pre[l]), mm(w_ple_gate[l]), mm(w_ple_proj[l]), row(ple_post[l]),
                         row(final_norm), tm=tm)

        outs["c_p"].append(c_lat[:tp].reshape(bp, seq, KV_LORA))
        outs["kr_p"].append(k_rope[:tp].reshape(bp, seq, QK_ROPE))
        outs["c_s"].append(c_lat[tp:].reshape(bs, 1, KV_LORA))
        outs["kr_s"].append(k_rope[tp:].reshape(bs, 1, QK_ROPE))
        outs["ssm_p"].append(ssm_p.reshape(bp, SSM_HEADS, SSM_HEADDIM, SSM_STATE))
        outs["ssm_s"].append(ssm_s.reshape(bs, SSM_HEADS, SSM_HEADDIM, SSM_STATE))
        outs["conv_p"].append(conv_p)
        outs["conv_s"].append(jnp.concatenate([state_conv[l][:, 1:, :], xbc_s[:, None, :]], axis=1))
        outs["mc_p"].append(mc_p.reshape(bp, MLSTM_HEADS, MLSTM_QK, MLSTM_V))
        outs["mc_s"].append(mc_s.reshape(bs, MLSTM_HEADS, MLSTM_QK, MLSTM_V))
        outs["mn_p"].append(mn_p[:, :MLSTM_HEADS, :])
        outs["mn_s"].append(jnp.swapaxes(mn_s, 0, 1))
        outs["mm_p"].append(mm_p[:, :MLSTM_HEADS, 0])
        outs["mm_s"].append(mm_s[:, :MLSTM_HEADS])

    st = {k: jnp.stack(v, axis=0) for k, v in outs.items()}
    return (y_final[:tp].reshape(bp, seq, d), y_final[tp:].reshape(bs, 1, d),
            st["c_p"], st["kr_p"], st["c_s"], st["kr_s"], st["ssm_p"], st["ssm_s"], st["conv_p"], st["conv_s"],
            st["mc_p"], st["mc_s"], st["mn_p"], st["mn_s"], st["mm_p"], st["mm_s"])
```
